```python
import math
import jax, jax.numpy as jnp
from jax import lax
import numpy as np

D_MODEL = 4096
BATCH = 2
SEQ = 4096
DEPTH = 2

GRID_W = 64
CTX_LEN = 256
HEAD_DIM = 128
ROPE_THETA = 10000.0
EPS = 1e-6
Q_BLOCK = 128
N_BRANCH = 4
BRANCH_W = D_MODEL // N_BRANCH
NA_HEADS = BRANCH_W // HEAD_DIM
NA_WIN_H = 8
NA_WIN_W = 16
GQA_Q_HEADS = BRANCH_W // HEAD_DIM
GQA_KV_HEADS = GQA_Q_HEADS // 4
KV_W = GQA_KV_HEADS * HEAD_DIM
FNET_GROUPS = 4
FNET_GROUP_W = BRANCH_W // FNET_GROUPS
DIFF_HEADS = BRANCH_W // (2 * HEAD_DIM)
D_FF = 6144
N_MOD = 9
COL_SIZES = (BRANCH_W, BRANCH_W, BRANCH_W,
             BRANCH_W, KV_W, KV_W,
             BRANCH_W,
             BRANCH_W, BRANCH_W, BRANCH_W,
             N_BRANCH * D_MODEL)
W_IN_COLS = sum(COL_SIZES)

kernel_name = 'hybrid_gated_mixer_dit_block'


def _rms(x, g):
    xf = x.astype(jnp.float32)
    y = xf * lax.rsqrt(jnp.mean(xf * xf, axis=-1, keepdims=True) + EPS)
    return (y * g.astype(jnp.float32)).astype(x.dtype)


def _modulate(h, shift, scale):
    return h * (1 + scale) + shift


def _swiglu(h, w_in, w_out):
    gate, up = jnp.split(h @ w_in, 2, axis=-1)
    return (jax.nn.silu(gate) * up) @ w_out


def _rope_tables(n_tok):
    t = jnp.arange(n_tok, dtype=jnp.int32)
    row = (t // GRID_W).astype(jnp.float32)
    col = (t % GRID_W).astype(jnp.float32)
    axis_dim = HEAD_DIM // 2
    inv = ROPE_THETA ** (-jnp.arange(0, axis_dim, 2, dtype=jnp.float32) / axis_dim)
    ang = jnp.stack([row[:, None] * inv, col[:, None] * inv], axis=1)
    return jnp.cos(ang), jnp.sin(ang)


def _rope(x, cos, sin):
    shp = x.shape
    xf = x.astype(jnp.float32).reshape(shp[:-1] + (2, 2, HEAD_DIM // 4))
    x1, x2 = xf[..., 0, :], xf[..., 1, :]
    c, s = cos[:, None], sin[:, None]
    out = jnp.stack([x1 * c - x2 * s, x2 * c + x1 * s], axis=-2)
    return out.reshape(shp).astype(x.dtype)


def _project_heads(h, w_in, qn_a, kn_a, qn_b, kn_b, qn_d, kn_d, cos, sin):
    B, T, _ = h.shape
    cuts = np.cumsum(COL_SIZES)[:-1].tolist()
    aq, ak, av, bq, bk, bv, cu, dq, dk, dv, gates = jnp.split(h @ w_in, cuts, axis=-1)
    aq = _rms(aq.reshape(B, T, NA_HEADS, HEAD_DIM), qn_a)
    ak = _rms(ak.reshape(B, T, NA_HEADS, HEAD_DIM), kn_a)
    av = av.reshape(B, T, NA_HEADS, HEAD_DIM)
    bq = _rms(bq.reshape(B, T, GQA_Q_HEADS, HEAD_DIM), qn_b)
    bk = _rms(bk.reshape(B, T, GQA_KV_HEADS, HEAD_DIM), kn_b)
    bv = bv.reshape(B, T, GQA_KV_HEADS, HEAD_DIM)
    dq = _rms(dq.reshape(B, T, 2 * DIFF_HEADS, HEAD_DIM), qn_d)
    dk = _rms(dk.reshape(B, T, 2 * DIFF_HEADS, HEAD_DIM), kn_d)
    dv = dv.reshape(B, T, DIFF_HEADS, 2 * HEAD_DIM)
    if cos is not None:
        bq, bk, dq, dk = _rope(bq, cos, sin), _rope(bk, cos, sin), _rope(dq, cos, sin), _rope(dk, cos, sin)
    dq = dq.reshape(B, T, DIFF_HEADS, 2, HEAD_DIM)
    dk = dk.reshape(B, T, DIFF_HEADS, 2, HEAD_DIM)
    return aq, ak, av, bq, bk, bv, cu, dq, dk, dv, gates


def _neighbourhood_attn(q, k, v, kc, vc, rpb):
    B, T, H, d = q.shape
    rows = T // GRID_W
    wh = min(NA_WIN_H, rows)
    r = jnp.arange(rows)
    row_idx = jnp.clip(r - wh // 2, 0, rows - wh)[:, None] + jnp.arange(wh)[None, :]
    cq = jnp.arange(GRID_W)
    start_c = jnp.clip(cq - NA_WIN_W // 2, 0, GRID_W - NA_WIN_W)
    in_win = (cq[None, :] >= start_c[:, None]) & (cq[None, :] < start_c[:, None] + NA_WIN_W)
    dr = row_idx - r[:, None]
    dc = jnp.clip(cq[None, :] - cq[:, None], -(NA_WIN_W - 1), NA_WIN_W - 1)
    bias = rpb.astype(jnp.float32)[:, (dr + NA_WIN_H - 1)[:, None, :, None],
                                   (dc + NA_WIN_W - 1)[None, :, None, :]]
    bias = jnp.where(in_win[None, None, :, None, :], bias, -jnp.inf)
    nk = wh * GRID_W
    bias = bias.reshape(H, rows, GRID_W, nk).transpose(1, 0, 2, 3)
    qg = q.reshape(B, rows, GRID_W, H, d)
    kg = k.reshape(B, rows, GRID_W, H, d)[:, row_idx].reshape(B, rows, nk, H, d)
    vg = v.reshape(B, rows, GRID_W, H, d)[:, row_idx].reshape(B, rows, nk, H, d)
    scale = d ** -0.5
    s_loc = jnp.einsum('brqhd,brkhd->brhqk', qg, kg).astype(jnp.float32) * scale + bias[None]
    s_ctx = jnp.einsum('brqhd,bkhd->brhqk', qg, kc).astype(jnp.float32) * scale
    p = jax.nn.softmax(jnp.concatenate([s_loc, s_ctx], axis=-1), axis=-1).astype(v.dtype)
    o = (jnp.einsum('brhqk,brkhd->brqhd', p[..., :nk], vg)
         + jnp.einsum('brhqk,bkhd->brqhd', p[..., nk:], vc))
    return o.reshape(B, T, H * d)


def _sdpa_blocked(q, k, v):
    B, T, hq, d = q.shape
    hkv = k.shape[2]
    grp = hq // hkv
    nb = T // Q_BLOCK
    qb = q.reshape(B, nb, Q_BLOCK, hkv, grp, d).transpose(1, 0, 2, 3, 4, 5)
    scale = d ** -0.5

    def block(qi):
        s = jnp.einsum('bqhgd,bkhd->bhgqk', qi, k).astype(jnp.float32) * scale
        p = jax.nn.softmax(s, axis=-1).astype(v.dtype)
        return jnp.einsum('bhgqk,bkhe->bqhge', p, v)

    o = lax.map(block, qb)
    return o.transpose(1, 0, 2, 3, 4, 5).reshape(B, T, hq * v.shape[-1])


def _diff_blocked(q, k, v, lam):
    B, T, H, _, d = q.shape
    nb = T // Q_BLOCK
    qb = q.reshape(B, nb, Q_BLOCK, H, 2, d).transpose(1, 0, 2, 3, 4, 5)
    scale = d ** -0.5

    def block(qi):
        s = jnp.einsum('bqhcd,bkhcd->bhcqk', qi, k).astype(jnp.float32) * scale
        p = jax.nn.softmax(s, axis=-1)
        w = (p[:, :, 0] - lam * p[:, :, 1]).astype(v.dtype)
        return jnp.einsum('bhqk,bkhe->bqhe', w, v)

    o = lax.map(block, qb)
    return o.transpose(1, 0, 2, 3, 4).reshape(B, T, H, v.shape[-1])


def _diff_out(o, subln, lam_init):
    B, T = o.shape[:2]
    return (_rms(o, subln) * (1.0 - lam_init)).reshape(B, T, DIFF_HEADS * 2 * HEAD_DIM)


def _fourier(u):
    B, T, _ = u.shape
    ug = u.astype(jnp.float32).reshape(B, T, FNET_GROUPS, FNET_GROUP_W)
    f = jnp.fft.fft2(ug, axes=(1, 3), norm='ortho').real
    return f.reshape(B, T, BRANCH_W).astype(u.dtype)


def _merge(ya, yb, yc, yd, gates, w_br, w_out):
    B, T, _ = gates.shape
    g = jax.nn.sigmoid(gates.reshape(B, T, N_BRANCH, D_MODEL))
    proj = jnp.einsum('btnc,ncd->btnd', jnp.stack([ya, yb, yc, yd], axis=2), w_br)
    return jnp.sum(g * proj, axis=2) @ w_out


def _layer(x, xc, mod, mod_c, cos, sin, lam_init, ctx_out,
           norm1, norm2, norm3, ffn1_in, ffn1_out, ffn2_in, ffn2_out, w_in, w_br, w_out,
           qn_a, kn_a, rpb_a, qn_b, kn_b, qn_d, kn_d, lam_q1, lam_k1, lam_q2, lam_k2, subln_d):
    sh1, sc1, g1, sh2, sc2, g2, sh3, sc3, g3 = jnp.split(mod, N_MOD, axis=-1)
    csh1, csc1, cg1, csh2, csc2, cg2, csh3, csc3, cg3 = jnp.split(mod_c, N_MOD, axis=-1)
    x = x + 0.5 * g1 * _swiglu(_modulate(_rms(x, norm1), sh1, sc1), ffn1_in, ffn1_out)
    xc = xc + 0.5 * cg1 * _swiglu(_modulate(_rms(xc, norm1), csh1, csc1), ffn1_in, ffn1_out)
    h = _modulate(_rms(x, norm2), sh2, sc2)
    hc = _modulate(_rms(xc, norm2), csh2, csc2)
    aq, ak, av, bq, bk, bv, cu, dq, dk, dv, gates = _project_heads(
        h, w_in, qn_a, kn_a, qn_b, kn_b, qn_d, kn_d, cos, sin)
    caq, cak, cav, cbq, cbk, cbv, ccu, cdq, cdk, cdv, cgates = _project_heads(
        hc, w_in, qn_a, kn_a, qn_b, kn_b, qn_d, kn_d, None, None)
    lq1, lk1 = lam_q1.astype(jnp.float32), lam_k1.astype(jnp.float32)
    lq2, lk2 = lam_q2.astype(jnp.float32), lam_k2.astype(jnp.float32)
    lam = jnp.exp(jnp.sum(lq1 * lk1)) - jnp.exp(jnp.sum(lq2 * lk2)) + lam_init
    ya = _neighbourhood_attn(aq, ak, av, cak, cav, rpb_a)
    yb = _sdpa_blocked(bq, jnp.concatenate([bk, cbk], axis=1), jnp.concatenate([bv, cbv], axis=1))
    yc = _fourier(cu)
    yd = _diff_out(_diff_blocked(dq, jnp.concatenate([dk, cdk], axis=1),
                                 jnp.concatenate([dv, cdv], axis=1), lam), subln_d, lam_init)
    x = x + g2 * _merge(ya, yb, yc, yd, gates, w_br, w_out)
    if ctx_out:
        cya = _sdpa_blocked(caq, cak, cav)
        cyb = _sdpa_blocked(cbq, cbk, cbv)
        cyc = _fourier(ccu)
        cyd = _diff_out(_diff_blocked(cdq, cdk, cdv, lam), subln_d, lam_init)
        xc = xc + cg2 * _merge(cya, cyb, cyc, cyd, cgates, w_br, w_out)
    x = x + 0.5 * g3 * _swiglu(_modulate(_rms(x, norm3), sh3, sc3), ffn2_in, ffn2_out)
    if ctx_out:
        xc = xc + 0.5 * cg3 * _swiglu(_modulate(_rms(xc, norm3), csh3, csc3), ffn2_in, ffn2_out)
    return x, xc


def setup_inputs(seed: int = 0) -> dict:
    key = jax.random.key(seed)
    ks = jax.random.split(key, 28)
    f32 = jnp.float32
    D, L, F = D_MODEL, DEPTH, D_FF

    def nrm(k, shape, scale):
        return jax.random.normal(k, shape, f32) * scale

    return {
        'x': nrm(ks[0], (BATCH, SEQ, D), 1.0),
        'c': nrm(ks[1], (BATCH, D), 1.0),
        'ctx': nrm(ks[2], (BATCH, CTX_LEN, D), 1.0),
        'c_ctx': nrm(ks[3], (D,), 1.0),
        'norm1': 1.0 + nrm(ks[4], (L, D), 0.02),
        'norm2': 1.0 + nrm(ks[5], (L, D), 0.02),
        'norm3': 1.0 + nrm(ks[6], (L, D), 0.02),
        'w_ada': nrm(ks[7], (L, D, N_MOD * D), 0.5 * D ** -0.5),
        'b_ada': nrm(ks[8], (L, N_MOD * D), 0.02),
        'ffn1_in': nrm(ks[9], (L, D, 2 * F), D ** -0.5),
        'ffn1_out': nrm(ks[10], (L, F, D), F ** -0.5),
        'ffn2_in': nrm(ks[11], (L, D, 2 * F), D ** -0.5),
        'ffn2_out': nrm(ks[12], (L, F, D), F ** -0.5),
        'w_in': nrm(ks[13], (L, D, W_IN_COLS), D ** -0.5),
        'w_br': nrm(ks[14], (L, N_BRANCH, BRANCH_W, D), BRANCH_W ** -0.5),
        'w_out': nrm(ks[15], (L, D, D), D ** -0.5),
        'qn_a': 1.0 + nrm(ks[16], (L, HEAD_DIM), 0.02),
        'kn_a': 1.0 + nrm(ks[17], (L, HEAD_DIM), 0.02),
        'rpb_a': nrm(ks[18], (L, NA_HEADS, 2 * NA_WIN_H - 1, 2 * NA_WIN_W - 1), 0.5),
        'qn_b': 1.0 + nrm(ks[19], (L, HEAD_DIM), 0.02),
        'kn_b': 1.0 + nrm(ks[20], (L, HEAD_DIM), 0.02),
        'qn_d': 1.0 + nrm(ks[21], (L, HEAD_DIM), 0.02),
        'kn_d': 1.0 + nrm(ks[22], (L, HEAD_DIM), 0.02),
        'lam_q1': nrm(ks[23], (L, HEAD_DIM), 0.1),
        'lam_k1': nrm(ks[24], (L, HEAD_DIM), 0.1),
        'lam_q2': nrm(ks[25], (L, HEAD_DIM), 0.1),
        'lam_k2': nrm(ks[26], (L, HEAD_DIM), 0.1),
        'subln_d': 1.0 + nrm(ks[27], (L, 2 * HEAD_DIM), 0.02),
    }


def reference(x, c, ctx, c_ctx, norm1, norm2, norm3, w_ada, b_ada,
              ffn1_in, ffn1_out, ffn2_in, ffn2_out, w_in, w_br, w_out,
              qn_a, kn_a, rpb_a, qn_b, kn_b, qn_d, kn_d,
              lam_q1, lam_k1, lam_q2, lam_k2, subln_d):
    cos, sin = _rope_tables(x.shape[1])
    sc = jax.nn.silu(c)
    scc = jax.nn.silu(c_ctx)
    xc = ctx
    for l in range(DEPTH):
        mod = (sc @ w_ada[l] + b_ada[l])[:, None, :]
        mod_c = scc @ w_ada[l] + b_ada[l]
        lam_init = 0.8 - 0.6 * math.exp(-0.3 * l)
        x, xc = _layer(x, xc, mod, mod_c, cos, sin, lam_init, l < DEPTH - 1,
                       norm1[l], norm2[l], norm3[l], ffn1_in[l], ffn1_out[l], ffn2_in[l], ffn2_out[l],
                       w_in[l], w_br[l], w_out[l], qn_a[l], kn_a[l], rpb_a[l], qn_b[l], kn_b[l],
                       qn_d[l], kn_d[l], lam_q1[l], lam_k1[l], lam_q2[l], lam_k2[l], subln_d[l])
    return x
```

```python
import functools
import math

import jax
import jax.numpy as jnp
from jax import lax
from jax.experimental import pallas as pl
from jax.experimental.pallas import tpu as pltpu

GRID_W = 64
FNET_GROUPS = 4
ROPE_THETA = 10000.0
EPS = 1e-6
N_MOD = 9

V7X_LANES = 128
V7X_VMEM_BYTES = 64 * 1024 * 1024
VMEM_LIMIT_BYTES = V7X_VMEM_BYTES - 8 * 1024 * 1024

BF16 = jnp.bfloat16
F32 = jnp.float32
_NT = (((1,), (1,)), ((), ()))


def _tile(dim, pref, align):
    if dim <= pref:
        return dim
    t = (pref // align) * align
    while t >= align:
        if dim % t == 0:
            return t
        t -= align
    return dim


def _params(n_grid):
    return pltpu.CompilerParams(
        dimension_semantics=("arbitrary",) * n_grid,
        vmem_limit_bytes=VMEM_LIMIT_BYTES)


def _resident(shape, index_map):
    return pl.BlockSpec(shape, index_map, pipeline_mode=pl.Buffered(1))


def _ada_kernel(c_ref, w_ref, b_ref, o_ref):
    c = c_ref[...]
    s = (c / (1.0 + jnp.exp(-c))).astype(BF16)
    o_ref[...] = jnp.dot(s, w_ref[...].astype(BF16),
                         preferred_element_type=F32) + b_ref[...]


def _ada(c_rows, w_ada, b_ada):
    n_layers, d, n = w_ada.shape
    r = c_rows.shape[0]
    tn = _tile(n, 512, V7X_LANES)
    return pl.pallas_call(
        _ada_kernel,
        grid=(n_layers, n // tn),
        in_specs=[pl.BlockSpec((r, d), lambda l, j: (0, 0)),
                  pl.BlockSpec((None, d, tn), lambda l, j: (l, 0, j)),
                  pl.BlockSpec((None, 1, tn), lambda l, j: (l, 0, j))],
        out_specs=pl.BlockSpec((None, r, tn), lambda l, j: (l, 0, j)),
        out_shape=jax.ShapeDtypeStruct((n_layers, r, n), F32),
        compiler_params=_params(2), name="ada_mod",
    )(c_rows, w_ada, b_ada.reshape(n_layers, 1, n))


def _norm_mod_kernel(x_ref, g_ref, sh_ref, sc_ref, o_ref):
    x = x_ref[...]
    y = x * lax.rsqrt(jnp.mean(x * x, axis=-1, keepdims=True) + EPS) * g_ref[...]
    o_ref[...] = (y * (1.0 + sc_ref[...]) + sh_ref[...]).astype(o_ref.dtype)


def _norm_mod(x, g, mod, k_shift, k_scale, row_of_tile):
    r, d = x.shape
    tr = _tile(r, 256, 8)
    return pl.pallas_call(
        _norm_mod_kernel,
        grid=(r // tr,),
        in_specs=[pl.BlockSpec((tr, d), lambda i: (i, 0)),
                  pl.BlockSpec((1, d), lambda i: (0, 0)),
                  pl.BlockSpec((None, 1, d), lambda i: (row_of_tile(i, tr), 0, k_shift)),
                  pl.BlockSpec((None, 1, d), lambda i: (row_of_tile(i, tr), 0, k_scale))],
        out_specs=pl.BlockSpec((tr, d), lambda i: (i, 0)),
        out_shape=jax.ShapeDtypeStruct((r, d), BF16),
        compiler_params=_params(1), name="norm_mod",
    )(x, g.reshape(1, d), mod, mod)


def _ffn_in_kernel(h_ref, wg_ref, wu_ref, o_ref):
    h = h_ref[...]
    g = jnp.dot(h, wg_ref[...], preferred_element_type=F32)
    u = jnp.dot(h, wu_ref[...], preferred_element_type=F32)
    o_ref[...] = (g / (1.0 + jnp.exp(-g)) * u).astype(o_ref.dtype)


def _ffn_in(h, w):
    m, d = h.shape
    f = w.shape[1] // 2
    bm = _tile(m, 1024, 16)
    bf = _tile(f, 512, V7X_LANES)
    nf = f // bf
    return pl.pallas_call(
        _ffn_in_kernel,
        grid=(m // bm, nf),
        in_specs=[_resident((bm, d), lambda i, j: (i, 0)),
                  pl.BlockSpec((d, bf), lambda i, j: (0, j)),
                  pl.BlockSpec((d, bf), lambda i, j: (0, j + nf))],
        out_specs=pl.BlockSpec((bm, bf), lambda i, j: (i, j)),
        out_shape=jax.ShapeDtypeStruct((m, f), BF16),
        compiler_params=_params(2), name="ffn_in",
    )(h, w, w)


def _mm_res_kernel(a_ref, w_ref, x_ref, g_ref, o_ref, *, coef):
    y = jnp.dot(a_ref[...], w_ref[...], preferred_element_type=F32)
    o_ref[...] = x_ref[...] + coef * g_ref[...] * y


def _mm_res(a, w, x, mod, k_gate, row_of_tile, coef):
    m, k = a.shape
    n = w.shape[1]
    bm = _tile(m, 1024, 16)
    bn = _tile(n, 512, V7X_LANES)
    return pl.pallas_call(
        functools.partial(_mm_res_kernel, coef=coef),
        grid=(m // bm, n // bn),
        in_specs=[_resident((bm, k), lambda i, j: (i, 0)),
                  pl.BlockSpec((k, bn), lambda i, j: (0, j)),
                  pl.BlockSpec((bm, bn), lambda i, j: (i, j)),
                  pl.BlockSpec((None, 1, bn),
                               lambda i, j: (row_of_tile(i, bm), 0, k_gate * (n // bn) + j))],
        out_specs=pl.BlockSpec((bm, bn), lambda i, j: (i, j)),
        out_shape=jax.ShapeDtypeStruct((m, n), F32),
        compiler_params=_params(2), name="mm_res",
    )(a, w, x, mod)


def _proj_kernel(h_ref, w_ref, *rest, kind, head_dim):
    o_ref = rest[-1]
    y = jnp.dot(h_ref[...], w_ref[...], preferred_element_type=F32)
    if kind == "plain":
        o_ref[...] = y.astype(o_ref.dtype)
        return
    g_ref = rest[0]
    if kind == "rope":
        cos = rest[1][...]
        sin_lo = rest[2][...]
        sin_hi = rest[3][...]
    for c in range(y.shape[1] // head_dim):
        sl = slice(c * head_dim, (c + 1) * head_dim)
        yc = y[:, sl]
        yc = yc * lax.rsqrt(jnp.mean(yc * yc, axis=-1, keepdims=True) + EPS) * g_ref[:, sl]
        if kind == "rope":
            q = head_dim // 4
            yc = (yc * cos + pltpu.roll(yc, q, 1) * sin_hi
                  + pltpu.roll(yc, head_dim - q, 1) * sin_lo)
        o_ref[:, sl] = yc.astype(o_ref.dtype)


def _proj(h, w, kind, gains=None, rope=None, tokens_per_batch=None, head_dim=128):
    m, d = h.shape
    n = w.shape[1]
    bm = _tile(m if tokens_per_batch is None else tokens_per_batch, 1024, 16)
    bn = _tile(n, 512, 2 * V7X_LANES)
    in_specs = [_resident((bm, d), lambda i, j: (i, 0)),
                pl.BlockSpec((d, bn), lambda i, j: (0, j))]
    args = [h, w]
    if kind != "plain":
        in_specs.append(pl.BlockSpec((1, bn), lambda i, j: (0, j)))
        args.append(gains.reshape(1, n))
    if kind == "rope":
        tiles_per_batch = tokens_per_batch // bm
        for tab in rope:
            in_specs.append(_resident((bm, head_dim), lambda i, j: (i % tiles_per_batch, 0)))
            args.append(tab)
    return pl.pallas_call(
        functools.partial(_proj_kernel, kind=kind, head_dim=head_dim),
        grid=(m // bm, n // bn),
        in_specs=in_specs,
        out_specs=pl.BlockSpec((bm, bn), lambda i, j: (i, j)),
        out_shape=jax.ShapeDtypeStruct((m, n), BF16),
        compiler_params=_params(2), name="proj_" + kind,
    )(*args)


def _softmax_parts(q, k_refs, scale, col=None):
    ss = []
    for k_ref in k_refs:
        k = k_ref[...] if col is None else k_ref[:, col]
        ss.append(lax.dot_general(q, k, _NT, preferred_element_type=F32) * scale)
    m = ss[0].max(axis=-1, keepdims=True)
    for s in ss[1:]:
        m = jnp.maximum(m, s.max(axis=-1, keepdims=True))
    es = [jnp.exp(s - m) for s in ss]
    l = es[0].sum(axis=-1, keepdims=True)
    for e in es[1:]:
        l = l + e.sum(axis=-1, keepdims=True)
    return es, l


def _sdpa_kernel(q_ref, *rest, n_src, grp, head_dim, scale):
    o_ref = rest[-1]
    k_refs = rest[0:2 * n_src:2]
    v_refs = rest[1:2 * n_src:2]
    tq = q_ref.shape[0]
    q = q_ref[...]
    if grp > 1:
        q = jnp.concatenate(
            [q[:, g * head_dim:(g + 1) * head_dim] for g in range(grp)], axis=0)
    es, l = _softmax_parts(q, k_refs, scale)
    o = jnp.dot(es[0].astype(BF16), v_refs[0][...], preferred_element_type=F32)
    for e, v_ref in zip(es[1:], v_refs[1:]):
        o = o + jnp.dot(e.astype(BF16), v_ref[...], preferred_element_type=F32)
    o = o * (1.0 / l)
    for g in range(grp):
        o_ref[:, g * head_dim:(g + 1) * head_dim] = (
            o[g * tq:(g + 1) * tq].astype(o_ref.dtype))


def _sdpa(q, q_col0, srcs, n_batch, n_kv, grp, head_dim, tq_pref):
    tq_all = q.shape[0] // n_batch
    tq = _tile(tq_all, tq_pref, 16)
    nq = tq_all // tq
    in_specs = [pl.BlockSpec((tq, grp * head_dim), lambda b, h, i: (b * nq + i, q_col0 + h))]
    args = [q]
    for k_arr, k_col0, v_arr, v_col0 in srcs:
        tk = k_arr.shape[0] // n_batch
        in_specs.append(pl.BlockSpec((tk, head_dim), lambda b, h, i, c=k_col0: (b, c + h)))
        in_specs.append(pl.BlockSpec((tk, head_dim), lambda b, h, i, c=v_col0: (b, c + h)))
        args += [k_arr, v_arr]
    return pl.pallas_call(
        functools.partial(_sdpa_kernel, n_src=len(srcs), grp=grp, head_dim=head_dim,
                          scale=head_dim ** -0.5),
        grid=(n_batch, n_kv, nq),
        in_specs=in_specs,
        out_specs=pl.BlockSpec((tq, grp * head_dim), lambda b, h, i: (b * nq + i, h)),
        out_shape=jax.ShapeDtypeStruct((q.shape[0], n_kv * grp * head_dim), BF16),
        compiler_params=_params(3), name="sdpa",
    )(*args)


def _diff_kernel(q_ref, *rest, n_src, head_dim, scale, lam_init):
    o_ref = rest[-1]
    lq1, lk1, lq2, lk2, subln = (r[...] for r in rest[2 * n_src:2 * n_src + 5])
    k_refs = rest[0:2 * n_src:2]
    v_refs = rest[1:2 * n_src:2]
    lam = (jnp.exp(jnp.sum(lq1 * lk1, axis=-1, keepdims=True))
           - jnp.exp(jnp.sum(lq2 * lk2, axis=-1, keepdims=True)) + lam_init)
    q = q_ref[...]
    e1, l1 = _softmax_parts(q[:, :head_dim], k_refs, scale, slice(0, head_dim))
    e2, l2 = _softmax_parts(q[:, head_dim:], k_refs, scale, slice(head_dim, 2 * head_dim))
    r1 = 1.0 / l1
    r2 = lam / l2
    o = None
    for a, b, v_ref in zip(e1, e2, v_refs):
        w = (a * r1 - b * r2).astype(BF16)
        t = jnp.dot(w, v_ref[...], preferred_element_type=F32)
        o = t if o is None else o + t
    y = o * lax.rsqrt(jnp.mean(o * o, axis=-1, keepdims=True) + EPS) * subln
    o_ref[...] = (y * (1.0 - lam_init)).astype(o_ref.dtype)


def _diff(q, q_col0, srcs, lam_vecs, subln, lam_init, n_batch, n_heads, head_dim, tq_pref):
    tq_all = q.shape[0] // n_batch
    tq = _tile(tq_all, tq_pref, 16)
    nq = tq_all // tq
    w = 2 * head_dim
    in_specs = [pl.BlockSpec((tq, w), lambda b, h, i: (b * nq + i, q_col0 + h))]
    args = [q]
    for k_arr, k_col0, v_arr, v_col0 in srcs:
        tk = k_arr.shape[0] // n_batch
        in_specs.append(pl.BlockSpec((tk, w), lambda b, h, i, c=k_col0: (b, c + h)))
        in_specs.append(pl.BlockSpec((tk, w), lambda b, h, i, c=v_col0: (b, c + h)))
        args += [k_arr, v_arr]
    for vec in lam_vecs:
        in_specs.append(pl.BlockSpec((1, head_dim), lambda b, h, i: (0, 0)))
        args.append(vec.reshape(1, head_dim))
    in_specs.append(pl.BlockSpec((1, w), lambda b, h, i: (0, 0)))
    args.append(subln.reshape(1, w))
    return pl.pallas_call(
        functools.partial(_diff_kernel, n_src=len(srcs), head_dim=head_dim,
                          scale=head_dim ** -0.5, lam_init=lam_init),
        grid=(n_batch, n_heads, nq),
        in_specs=in_specs,
        out_specs=pl.BlockSpec((tq, w), lambda b, h, i: (b * nq + i, h)),
        out_shape=jax.ShapeDtypeStruct((q.shape[0], n_heads * w), BF16),
        compiler_params=_params(3), name="diff_attn",
    )(*args)


def _na_bias_kernel(rpb_ref, o_ref, *, win_w):
    n = o_ref.shape[1]
    lane = lax.broadcasted_iota(jnp.int32, (1, n), 1)
    cq = lax.shift_right_logical(lane, GRID_W.bit_length() - 1)
    ck = lane & (GRID_W - 1)
    start = jnp.clip(cq - win_w // 2, 0, GRID_W - win_w)
    in_win = (ck >= start) & (ck < start + win_w)
    dci = jnp.clip(ck - cq, -(win_w - 1), win_w - 1) + (win_w - 1)
    acc = jnp.zeros(o_ref.shape, F32)
    for dd in range(2 * win_w - 1):
        acc = jnp.where(dci == dd, rpb_ref[:, dd:dd + 1], acc)
    o_ref[...] = jnp.where(in_win, acc, -jnp.inf)


def _na_bias(rpb, win_h):
    n_heads, n_dr, n_dc = rpb.shape
    win_w = (n_dc + 1) // 2
    tb = pl.pallas_call(
        functools.partial(_na_bias_kernel, win_w=win_w),
        out_shape=jax.ShapeDtypeStruct((n_heads * n_dr, GRID_W * GRID_W), F32),
        name="na_bias",
    )(rpb.reshape(n_heads * n_dr, n_dc))
    tb = tb.reshape(n_heads, n_dr, GRID_W, GRID_W)
    pats = []
    for p in range(win_h):
        band = tb[:, win_h - 1 - p:2 * win_h - 1 - p]
        pats.append(band.transpose(0, 2, 1, 3).reshape(n_heads, GRID_W, win_h * GRID_W))
    return jnp.stack(pats, axis=1)


def _na_kernel(q_ref, k_ref, v_ref, kc_ref, vc_ref, bias_ref, o_ref, *,
               rows_per_step, n_rows, win_h, scale):
    rg = pl.program_id(2)
    kc = kc_ref[...]
    vc = vc_ref[...]
    band = win_h * GRID_W
    for rr in range(rows_per_step):
        r = rg * rows_per_step + rr
        start = jnp.clip(r - win_h // 2, 0, n_rows - win_h)
        off = pl.multiple_of(start * GRID_W, GRID_W)
        kb = k_ref[pl.ds(off, band), :]
        vb = v_ref[pl.ds(off, band), :]
        q = q_ref[rr * GRID_W:(rr + 1) * GRID_W, :]
        s1 = lax.dot_general(q, kb, _NT, preferred_element_type=F32) * scale + bias_ref[r - start]
        s2 = lax.dot_general(q, kc, _NT, preferred_element_type=F32) * scale
        m = jnp.maximum(s1.max(axis=-1, keepdims=True), s2.max(axis=-1, keepdims=True))
        e1 = jnp.exp(s1 - m)
        e2 = jnp.exp(s2 - m)
        l = e1.sum(axis=-1, keepdims=True) + e2.sum(axis=-1, keepdims=True)
        o = (jnp.dot(e1.astype(BF16), vb, preferred_element_type=F32)
             + jnp.dot(e2.astype(BF16), vc, preferred_element_type=F32))
        o_ref[rr * GRID_W:(rr + 1) * GRID_W, :] = (o * (1.0 / l)).astype(o_ref.dtype)


def _na(qk, v_arr, v_col0, cqk, cv_arr, cv_col0, bias, n_batch, n_heads, head_dim):
    t = qk.shape[0] // n_batch
    n_rows = t // GRID_W
    win_h = bias.shape[1]
    assert n_rows >= win_h
    rows_per_step = _tile(n_rows, 8, 1)
    n_rg = n_rows // rows_per_step
    tc = cqk.shape[0] // n_batch
    return pl.pallas_call(
        functools.partial(_na_kernel, rows_per_step=rows_per_step, n_rows=n_rows,
                          win_h=win_h, scale=head_dim ** -0.5),
        grid=(n_batch, n_heads, n_rg),
        in_specs=[
            pl.BlockSpec((rows_per_step * GRID_W, head_dim), lambda b, h, i: (b * n_rg + i, h)),
            pl.BlockSpec((t, head_dim), lambda b, h, i: (b, n_heads + h)),
            pl.BlockSpec((t, head_dim), lambda b, h, i: (b, v_col0 + h)),
            pl.BlockSpec((tc, head_dim), lambda b, h, i: (b, n_heads + h)),
            pl.BlockSpec((tc, head_dim), lambda b, h, i: (b, cv_col0 + h)),
            pl.BlockSpec((None, win_h, GRID_W, win_h * GRID_W), lambda b, h, i: (h, 0, 0, 0)),
        ],
        out_specs=pl.BlockSpec((rows_per_step * GRID_W, head_dim), lambda b, h, i: (b * n_rg + i, h)),
        out_shape=jax.ShapeDtypeStruct((qk.shape[0], n_heads * head_dim), BF16),
        compiler_params=_params(3), name="na_attn",
    )(qk, qk, v_arr, cqk, cv_arr, bias)


def _dft_tables(n):
    n2 = 1
    while n2 * n2 < n:
        n2 *= 2
    n1 = n // n2
    s = jnp.arange(n, dtype=jnp.int32)[None, :]

    def tab(t):
        ang = ((t[:, None] * s) % n).astype(F32) * (2.0 * math.pi / n)
        return jnp.cos(ang), jnp.sin(ang)

    ca, sa = tab(jnp.arange(n1, dtype=jnp.int32) * n2)
    cb, sb = tab(jnp.arange(n2, dtype=jnp.int32))
    ca, sa = ca[:, None, :], sa[:, None, :]
    cb, sb = cb[None, :, :], sb[None, :, :]
    c = (ca * cb - sa * sb).reshape(n, n)
    sn = (sa * cb + ca * sb).reshape(n, n)
    return c, sn


def _mm_plain_kernel(a_ref, w_ref, o_ref):
    o_ref[...] = jnp.dot(a_ref[...], w_ref[...], preferred_element_type=F32).astype(o_ref.dtype)


def _fourier_stage1(p_arr, col0, cs_w, n_groups):
    m = p_arr.shape[0]
    gw = cs_w.shape[0]
    bm = _tile(m, 2048, 16)
    return pl.pallas_call(
        _mm_plain_kernel,
        grid=(m // bm, n_groups),
        in_specs=[pl.BlockSpec((bm, gw), lambda i, g: (i, col0 + g)),
                  pl.BlockSpec((gw, 2 * gw), lambda i, g: (0, 0))],
        out_specs=pl.BlockSpec((bm, 2 * gw), lambda i, g: (i, g)),
        out_shape=jax.ShapeDtypeStruct((m, n_groups * 2 * gw), BF16),
        compiler_params=_params(2), name="fourier_chan",
    )(p_arr, cs_w)


def _fourier_stage2_kernel(c_ref, s_ref, a_ref, b_ref, o_ref, *, scale):
    y = (jnp.dot(c_ref[...], a_ref[...], preferred_element_type=F32)
         - jnp.dot(s_ref[...], b_ref[...], preferred_element_type=F32))
    o_ref[...] = (y * scale).astype(o_ref.dtype)


def _fourier_stage2(ab, cos_t, sin_t, n_batch, n_groups):
    m = ab.shape[0]
    t = m // n_batch
    gw = ab.shape[1] // (2 * n_groups)
    bm = _tile(t, 1024, 16)
    nt = t // bm
    scale = 1.0 / math.sqrt(t * gw)
    return pl.pallas_call(
        functools.partial(_fourier_stage2_kernel, scale=scale),
        grid=(nt, n_batch, n_groups),
        in_specs=[_resident((bm, t), lambda i, b, g: (i, 0)),
                  _resident((bm, t), lambda i, b, g: (i, 0)),
                  pl.BlockSpec((t, gw), lambda i, b, g: (b, 2 * g)),
                  pl.BlockSpec((t, gw), lambda i, b, g: (b, 2 * g + 1))],
        out_specs=pl.BlockSpec((bm, gw), lambda i, b, g: (b * nt + i, g)),
        out_shape=jax.ShapeDtypeStruct((m, n_groups * gw), BF16),
        compiler_params=_params(3), name="fourier_pos",
    )(cos_t, sin_t, ab, ab)


def _merge_kernel(h_ref, wg_ref, ya_ref, yb_ref, yc_ref, yd_ref, wb_ref, o_ref, acc_ref):
    n = pl.program_id(2)
    gate = jnp.dot(h_ref[...], wg_ref[...], preferred_element_type=F32)
    gate = 1.0 / (1.0 + jnp.exp(-gate))
    wb = wb_ref[...]

    @pl.when(n == 0)
    def _():
        acc_ref[...] = gate * jnp.dot(ya_ref[...], wb, preferred_element_type=F32)

    for idx, y_ref in ((1, yb_ref), (2, yc_ref), (3, yd_ref)):
        @pl.when(n == idx)
        def _(y_ref=y_ref):
            acc_ref[...] += gate * jnp.dot(y_ref[...], wb, preferred_element_type=F32)

    @pl.when(n == pl.num_programs(2) - 1)
    def _():
        o_ref[...] = acc_ref[...].astype(o_ref.dtype)


def _merge(h, w_gates, ys, w_br):
    m, d = h.shape
    n_br, bw, dout = w_br.shape
    assert n_br == 4 and len(ys) == 4
    bm = _tile(m, 1024, 16)
    bn = _tile(dout, 512, V7X_LANES)
    nj = dout // bn
    y_spec = _resident((bm, bw), lambda i, j, n: (i, 0))
    return pl.pallas_call(
        _merge_kernel,
        grid=(m // bm, nj, n_br),
        in_specs=[_resident((bm, d), lambda i, j, n: (i, 0)),
                  pl.BlockSpec((d, bn), lambda i, j, n: (0, n * nj + j)),
                  y_spec, y_spec, y_spec, y_spec,
                  pl.BlockSpec((None, bw, bn), lambda i, j, n: (n, 0, j))],
        out_specs=pl.BlockSpec((bm, bn), lambda i, j, n: (i, j)),
        out_shape=jax.ShapeDtypeStruct((m, dout), BF16),
        scratch_shapes=[pltpu.VMEM((bm, bn), F32)],
        compiler_params=_params(3), name="merge",
    )(h, w_gates, *ys, w_br)


def _rope_tables(t, head_dim):
    q = head_dim // 4
    tok = jnp.arange(t, dtype=jnp.int32)
    row = (tok // GRID_W).astype(F32)
    col = (tok % GRID_W).astype(F32)
    inv = ROPE_THETA ** (-jnp.arange(0, 2 * q, 2, dtype=F32) / (2 * q))
    ang = jnp.stack([row[:, None] * inv, col[:, None] * inv], axis=1)
    c, s = jnp.cos(ang), jnp.sin(ang)
    z = jnp.zeros_like(s)
    cos = jnp.stack([c, c], axis=2).reshape(t, head_dim)
    sin_lo = jnp.stack([-s, z], axis=2).reshape(t, head_dim)
    sin_hi = jnp.stack([z, s], axis=2).reshape(t, head_dim)
    return cos, sin_lo, sin_hi


def kernel(x, c, ctx, c_ctx, norm1, norm2, norm3, w_ada, b_ada, ffn1_in, ffn1_out, ffn2_in,
           ffn2_out, w_in, w_br, w_out, qn_a, kn_a, rpb_a, qn_b, kn_b, qn_d, kn_d,
           lam_q1, lam_k1, lam_q2, lam_k2, subln_d):
    n_batch, t, d = x.shape
    tc = ctx.shape[1]
    depth = w_ada.shape[0]
    hd = qn_a.shape[-1]
    bw = w_br.shape[2]
    n_heads = bw // hd
    kv_w = (w_in.shape[2] - 8 * bw - w_br.shape[1] * d) // 2
    n_kv = kv_w // hd
    grp = n_heads // n_kv
    n_diff = bw // (2 * hd)
    win_h = (rpb_a.shape[2] + 1) // 2
    gw = bw // FNET_GROUPS

    n_rows = -(-(n_batch + 1) // 8) * 8
    c_rows = jnp.zeros((n_rows, d), F32).at[:n_batch].set(c).at[n_batch].set(c_ctx)
    mod_all = _ada(c_rows, w_ada, b_ada).reshape(depth, n_rows, 1, N_MOD * d)

    def lat_row(i, tile):
        return (i * tile) // t

    def ctx_row(i, tile):
        return n_batch

    rope = _rope_tables(t, hd)
    cos_t, sin_t = (a.astype(BF16) for a in _dft_tables(t))
    cos_c, sin_c = (a.astype(BF16) for a in _dft_tables(tc))
    cos_g, sin_g = _dft_tables(gw)
    cs_g = jnp.concatenate([cos_g, sin_g], axis=1).astype(BF16)

    xl = x.reshape(n_batch * t, d)
    xc = ctx.reshape(n_batch * tc, d)
    for l in range(depth):
        mod = mod_all[l]
        lam_init = 0.8 - 0.6 * math.exp(-0.3 * l)
        ctx_out = l < depth - 1
        wi = w_in[l]
        o = 0
        cols = {}
        for name, width in (("aq", bw), ("ak", bw), ("av", bw), ("bq", bw), ("bk", kv_w),
                            ("bv", kv_w), ("cu", bw), ("dq", bw), ("dk", bw), ("dv", bw)):
            cols[name] = wi[:, o:o + width]
            o += width
        w_norm = jnp.concatenate([cols["aq"], cols["ak"]], axis=1).astype(BF16)
        w_rope = jnp.concatenate([cols["bq"], cols["dq"], cols["dk"], cols["bk"]], axis=1).astype(BF16)
        w_plain = jnp.concatenate([cols["av"], cols["cu"], cols["dv"], cols["bv"]], axis=1).astype(BF16)
        w_gates = wi[:, o:].astype(BF16)
        g_norm = jnp.concatenate([jnp.tile(qn_a[l], n_heads), jnp.tile(kn_a[l], n_heads)])
        g_rope = jnp.concatenate([jnp.tile(qn_b[l], n_heads), jnp.tile(qn_d[l], 2 * n_diff),
                                  jnp.tile(kn_d[l], 2 * n_diff), jnp.tile(kn_b[l], n_kv)])
        r_dq, r_dk, r_bk = bw, 2 * bw, 3 * bw
        p_cu, p_dv, p_bv = bw, 2 * bw, 3 * bw
        f1i, f1o = ffn1_in[l].astype(BF16), ffn1_out[l].astype(BF16)
        f2i, f2o = ffn2_in[l].astype(BF16), ffn2_out[l].astype(BF16)
        wbr, wo = w_br[l].astype(BF16), w_out[l].astype(BF16)

        h = _norm_mod(xl, norm1[l], mod, 0, 1, lat_row)
        xl = _mm_res(_ffn_in(h, f1i), f1o, xl, mod, 2, lat_row, 0.5)
        hc = _norm_mod(xc, norm1[l], mod, 0, 1, ctx_row)
        xc = _mm_res(_ffn_in(hc, f1i), f1o, xc, mod, 2, ctx_row, 0.5)

        h = _norm_mod(xl, norm2[l], mod, 3, 4, lat_row)
        hc = _norm_mod(xc, norm2[l], mod, 3, 4, ctx_row)
        qk_a = _proj(h, w_norm, "norm", g_norm, head_dim=hd)
        rp = _proj(h, w_rope, "rope", g_rope, rope, t, head_dim=hd)
        pp = _proj(h, w_plain, "plain", head_dim=hd)
        cqk_a = _proj(hc, w_norm, "norm", g_norm, head_dim=hd)
        crp = _proj(hc, w_rope, "norm", g_rope, head_dim=hd)
        cpp = _proj(hc, w_plain, "plain", head_dim=hd)

        bias = _na_bias(rpb_a[l], win_h)
        ya = _na(qk_a, pp, 0, cqk_a, cpp, 0, bias, n_batch, n_heads, hd)
        yb = _sdpa(rp, 0, [(rp, r_bk // hd, pp, p_bv // hd),
                           (crp, r_bk // hd, cpp, p_bv // hd)],
                   n_batch, n_kv, grp, hd, 128)
        ab = _fourier_stage1(pp, p_cu // gw, cs_g, FNET_GROUPS)
        yc = _fourier_stage2(ab, cos_t, sin_t, n_batch, FNET_GROUPS)
        lam_vecs = (lam_q1[l], lam_k1[l], lam_q2[l], lam_k2[l])
        w2 = 2 * hd
        yd = _diff(rp, r_dq // w2, [(rp, r_dk // w2, pp, p_dv // w2),
                                    (crp, r_dk // w2, cpp, p_dv // w2)],
                   lam_vecs, subln_d[l], lam_init, n_batch, n_diff, hd, 256)
        merged = _merge(h, w_gates, (ya, yb, yc, yd), wbr)
        xl = _mm_res(merged, wo, xl, mod, 5, lat_row, 1.0)

        if ctx_out:
            cya = _sdpa(cqk_a, 0, [(cqk_a, n_heads, cpp, 0)], n_batch, n_heads, 1, hd, 256)
            cyb = _sdpa(crp, 0, [(crp, r_bk // hd, cpp, p_bv // hd)], n_batch, n_kv, grp, hd, 256)
            cab = _fourier_stage1(cpp, p_cu // gw, cs_g, FNET_GROUPS)
            cyc = _fourier_stage2(cab, cos_c, sin_c, n_batch, FNET_GROUPS)
            cyd = _diff(crp, r_dq // w2, [(crp, r_dk // w2, cpp, p_dv // w2)],
                        lam_vecs, subln_d[l], lam_init, n_batch, n_diff, hd, 256)
            cmerged = _merge(hc, w_gates, (cya, cyb, cyc, cyd), wbr)
            xc = _mm_res(cmerged, wo, xc, mod, 5, ctx_row, 1.0)

        h = _norm_mod(xl, norm3[l], mod, 6, 7, lat_row)
        xl = _mm_res(_ffn_in(h, f2i), f2o, xl, mod, 8, lat_row, 0.5)
        if ctx_out:
            hc = _norm_mod(xc, norm3[l], mod, 6, 7, ctx_row)
            xc = _mm_res(_ffn_in(hc, f2i), f2o, xc, mod, 8, ctx_row, 0.5)
    return xl.reshape(n_batch, t, d)
```
